```python
import math
import jax, jax.numpy as jnp
from jax import lax
import numpy as np

D_MODEL = 2048
BATCH = 4
SEQ = 2048
DEPTH = 2
DEC_BATCH = 32
DEC_SEQ = 1
PAST_LEN = 8192
PAGE_SIZE = 128

HEAD_DIM = 128
A_WIDTH = D_MODEL // 2
A_HEADS = A_WIDTH // HEAD_DIM
A_KV_HEADS = A_HEADS // 2
A_HALF = HEAD_DIM // 2
B_HEADS = (D_MODEL - A_WIDTH) // HEAD_DIM
B_KV_HEADS = B_HEADS // 2
KV_HEADS = A_KV_HEADS + B_KV_HEADS
MIX_WIDTH = (A_HEADS + B_HEADS) * HEAD_DIM
Q_A = A_HEADS * HEAD_DIM
KV_A = A_KV_HEADS * HEAD_DIM
Q_B = B_HEADS * HEAD_DIM
KV_B = B_KV_HEADS * HEAD_DIM
IN_WIDTH = Q_A + 2 * KV_A + Q_B + 2 * KV_B
SPLITS = [Q_A, Q_A + KV_A, Q_A + 2 * KV_A, Q_A + 2 * KV_A + Q_B, Q_A + 2 * KV_A + Q_B + KV_B]
D_FF = 4 * D_MODEL
PLE_DIM = 256
MOBA_BLOCK = 256
MOBA_TOPK = 3
ATTN_Q_BLOCK = 128
MOBA_Q_CHUNK = 16
ROPE_THETA = 10000.0
EPS = 1e-6

kernel_name = 'hymba_diffattn_moba_decoder_step'


def _rmsnorm(x, g):
    xf = x.astype(jnp.float32)
    y = xf * lax.rsqrt(jnp.mean(xf * xf, axis=-1, keepdims=True) + EPS)
    return (y * g.astype(jnp.float32)).astype(x.dtype)


def _rope(x, pos):
    dim = x.shape[-1]
    half = dim // 2
    inv = jnp.exp(jnp.arange(half, dtype=jnp.float32) * (-2.0 * math.log(ROPE_THETA) / dim))
    ang = pos.astype(jnp.float32)[:, None] * inv[None, :]
    shape = (1, x.shape[1]) + (1,) * (x.ndim - 3) + (half,)
    cos = jnp.cos(ang).reshape(shape)
    sin = jnp.sin(ang).reshape(shape)
    xf = x.astype(jnp.float32)
    x1, x2 = xf[..., :half], xf[..., half:]
    return jnp.concatenate([x1 * cos - x2 * sin, x2 * cos + x1 * sin], axis=-1).astype(x.dtype)


def _diff_attention(q, k, v, pos0, lam):
    b, t, h = q.shape[:3]
    l, kvh = k.shape[1], k.shape[2]
    g = h // kvh
    qc = math.gcd(t, ATTN_Q_BLOCK)
    nc = t // qc
    qs = q.reshape(b, nc, qc, kvh, g, 2, A_HALF).transpose(1, 0, 2, 3, 4, 5, 6)
    key_pos = jnp.arange(l)
    scale = A_HALF ** -0.5

    def one_block(args):
        qb, c = args
        q_pos = pos0 + c * qc + jnp.arange(qc)
        s = jnp.einsum('bqkgmd,blkmd->bkgmql', qb, k, preferred_element_type=jnp.float32) * scale
        s = jnp.where(key_pos[None, :] <= q_pos[:, None], s, -jnp.inf)
        pr = jax.nn.softmax(s, axis=-1)
        w = pr[:, :, :, 0] - lam * pr[:, :, :, 1]
        o = jnp.einsum('bkgql,blkd->bqkgd', w.astype(v.dtype), v, preferred_element_type=jnp.float32)
        return o.astype(v.dtype)

    o = lax.map(one_block, (qs, jnp.arange(nc)))
    return o.transpose(1, 0, 2, 3, 4, 5).reshape(b, t, h, v.shape[-1])


def _moba_attention(q, k, v, pos0):
    b, t, h, d = q.shape
    l, kvh = k.shape[1], k.shape[2]
    g = h // kvh
    nb = -(-l // MOBA_BLOCK)
    pad = nb * MOBA_BLOCK - l
    kb = jnp.pad(k, ((0, 0), (0, pad), (0, 0), (0, 0))).reshape(b, nb, MOBA_BLOCK, kvh, d).transpose(0, 3, 1, 2, 4)
    vb = jnp.pad(v, ((0, 0), (0, pad), (0, 0), (0, 0))).reshape(b, nb, MOBA_BLOCK, kvh, d).transpose(0, 3, 1, 2, 4)
    k_mean = jnp.mean(kb.astype(jnp.float32), axis=3)
    qg = q.reshape(b, t, kvh, g, d)
    q_pos = pos0 + jnp.arange(t)
    own = q_pos // MOBA_BLOCK
    gate = jnp.einsum('btkgd,bknd->btkgn', qg.astype(jnp.float32), k_mean)
    cand = jnp.arange(nb)[None, :] < own[:, None]
    gate = jnp.where(cand[None, :, None, None, :], gate, -jnp.inf)
    n_sel = min(MOBA_TOPK, nb)
    top_val, top_idx = lax.top_k(gate, n_sel)
    own_b = jnp.broadcast_to(own[None, :, None, None, None], (b, t, kvh, g, 1)).astype(top_idx.dtype)
    idx = jnp.concatenate([top_idx, own_b], axis=-1)
    valid = jnp.concatenate([jnp.isfinite(top_val), jnp.ones((b, t, kvh, g, 1), dtype=bool)], axis=-1)
    ns = n_sel + 1
    qc = math.gcd(t, MOBA_Q_CHUNK)
    nc = t // qc

    def chunks(a):
        return a.reshape((b, nc, qc) + a.shape[2:]).transpose((1, 0, 2) + tuple(range(3, a.ndim + 1)))

    b_idx = jnp.arange(b)[:, None, None, None, None]
    h_idx = jnp.arange(kvh)[None, None, :, None, None]
    r_off = jnp.arange(MOBA_BLOCK)
    scale = d ** -0.5

    def one_chunk(args):
        qb, ib, vl, qp = args
        kg = kb[b_idx, h_idx, ib]
        vg = vb[b_idx, h_idx, ib]
        s = jnp.einsum('bqkgd,bqkgsrd->bqkgsr', qb, kg, preferred_element_type=jnp.float32) * scale
        k_pos = ib[..., None] * MOBA_BLOCK + r_off
        mask = vl[..., None] & (k_pos <= qp[None, :, None, None, None, None])
        s = jnp.where(mask, s, -jnp.inf)
        pr = jax.nn.softmax(s.reshape(s.shape[:4] + (ns * MOBA_BLOCK,)), axis=-1).reshape(s.shape)
        o = jnp.einsum('bqkgsr,bqkgsrd->bqkgd', pr.astype(vg.dtype), vg, preferred_element_type=jnp.float32)
        return o.astype(v.dtype)

    o = lax.map(one_chunk, (chunks(qg), chunks(idx), chunks(valid), q_pos.reshape(nc, qc)))
    return o.transpose(1, 0, 2, 3, 4, 5).reshape(b, t, h, d)


def _layer(x, p, pos0, past, lam_init, norm_attn, w_in, lam_q1, lam_k1, lam_q2, lam_k2, subln, w_out,
           norm_mlp, w_up, w_down, w_ple_gate, w_ple_proj, ple_norm):
    b, t, _ = x.shape
    pos = pos0 + jnp.arange(t)
    hn = _rmsnorm(x, norm_attn)
    qa, ka, va, qb, kb, vb = jnp.split(hn @ w_in, SPLITS, axis=-1)
    qa = _rope(qa.reshape(b, t, A_HEADS, 2, A_HALF), pos)
    ka = _rope(ka.reshape(b, t, A_KV_HEADS, 2, A_HALF), pos).reshape(b, t, A_KV_HEADS, HEAD_DIM)
    va = va.reshape(b, t, A_KV_HEADS, HEAD_DIM)
    qb = _rope(qb.reshape(b, t, B_HEADS, HEAD_DIM), pos)
    kb = _rope(kb.reshape(b, t, B_KV_HEADS, HEAD_DIM), pos)
    vb = vb.reshape(b, t, B_KV_HEADS, HEAD_DIM)
    new_k = jnp.concatenate([ka, kb], axis=2)
    new_v = jnp.concatenate([va, vb], axis=2)
    if past is None:
        ka_all, va_all, kb_all, vb_all = ka, va, kb, vb
    else:
        pka, pva, pkb, pvb = past
        ka_all = jnp.concatenate([pka, ka], axis=1)
        va_all = jnp.concatenate([pva, va], axis=1)
        kb_all = jnp.concatenate([pkb, kb], axis=1)
        vb_all = jnp.concatenate([pvb, vb], axis=1)
    f32 = jnp.float32
    lam = (jnp.exp(jnp.sum(lam_q1.astype(f32) * lam_k1.astype(f32)))
           - jnp.exp(jnp.sum(lam_q2.astype(f32) * lam_k2.astype(f32))) + lam_init)
    oa = _diff_attention(qa, ka_all.reshape(b, -1, A_KV_HEADS, 2, A_HALF), va_all, pos0, lam)
    oa = _rmsnorm(oa, subln) * (1.0 - lam_init)
    ob = _moba_attention(qb, kb_all, vb_all, pos0)
    x = x + jnp.concatenate([oa.reshape(b, t, -1), ob.reshape(b, t, -1)], axis=-1) @ w_out
    x = x + jnp.square(jax.nn.relu(_rmsnorm(x, norm_mlp) @ w_up)) @ w_down
    x = x + jax.nn.sigmoid(x @ w_ple_gate) * _rmsnorm(p @ w_ple_proj, ple_norm)
    return x, new_k, new_v


def setup_inputs(seed: int = 0) -> dict:
    key = jax.random.key(seed)
    ks = jax.random.split(key, 24)
    f32 = jnp.float32
    n_pages = PAST_LEN // PAGE_SIZE
    n_used = DEC_BATCH * n_pages
    n_pool = n_used + max(1, n_used // 4)

    def nrm(k, shape, scale):
        return jax.random.normal(k, shape, f32) * scale

    def gain(k, shape):
        return 1.0 + 0.05 * jax.random.normal(k, shape, f32)

    page_table = jax.random.permutation(ks[6], n_pool)[:n_used].reshape(DEC_BATCH, n_pages).astype(jnp.int32)
    return {
        'x_prompt': nrm(ks[0], (BATCH, SEQ, D_MODEL), 1.0),
        'x_sample': nrm(ks[1], (DEC_BATCH, DEC_SEQ, D_MODEL), 1.0),
        'p_prompt': nrm(ks[2], (DEPTH, BATCH, SEQ, PLE_DIM), 1.0),
        'p_sample': nrm(ks[3], (DEPTH, DEC_BATCH, DEC_SEQ, PLE_DIM), 1.0),
        'cache_k': nrm(ks[4], (DEPTH, n_pool, PAGE_SIZE, KV_HEADS, HEAD_DIM), 1.0),
        'cache_v': nrm(ks[5], (DEPTH, n_pool, PAGE_SIZE, KV_HEADS, HEAD_DIM), 1.0),
        'page_table': page_table,
        'norm_attn': gain(ks[7], (DEPTH, D_MODEL)),
        'w_in': nrm(ks[8], (DEPTH, D_MODEL, IN_WIDTH), D_MODEL ** -0.5),
        'lam_q1': nrm(ks[9], (DEPTH, A_HALF), 0.1),
        'lam_k1': nrm(ks[10], (DEPTH, A_HALF), 0.1),
        'lam_q2': nrm(ks[11], (DEPTH, A_HALF), 0.1),
        'lam_k2': nrm(ks[12], (DEPTH, A_HALF), 0.1),
        'subln': gain(ks[13], (DEPTH, HEAD_DIM)),
        'w_out': nrm(ks[14], (DEPTH, MIX_WIDTH, D_MODEL), MIX_WIDTH ** -0.5),
        'norm_mlp': gain(ks[15], (DEPTH, D_MODEL)),
        'w_up': nrm(ks[16], (DEPTH, D_MODEL, D_FF), D_MODEL ** -0.5),
        'w_down': nrm(ks[17], (DEPTH, D_FF, D_MODEL), D_FF ** -0.5),
        'w_ple_gate': nrm(ks[18], (DEPTH, D_MODEL, D_MODEL), D_MODEL ** -0.5),
        'w_ple_proj': nrm(ks[19], (DEPTH, PLE_DIM, D_MODEL), PLE_DIM ** -0.5),
        'ple_norm': gain(ks[20], (DEPTH, D_MODEL)),
        'final_norm': gain(ks[21], (D_MODEL,)),
    }


def reference(x_prompt, x_sample, p_prompt, p_sample, cache_k, cache_v, page_table, norm_attn, w_in,
              lam_q1, lam_k1, lam_q2, lam_k2, subln, w_out, norm_mlp, w_up, w_down, w_ple_gate,
              w_ple_proj, ple_norm, final_norm):
    dec_b, n_pages = page_table.shape
    past_len = n_pages * cache_k.shape[2]
    hp, hs = x_prompt, x_sample
    kp_list, vp_list, ks_list, vs_list = [], [], [], []
    for l in range(DEPTH):
        lam_init = 0.8 - 0.6 * math.exp(-0.3 * l)
        hp, kp, vp = _layer(hp, p_prompt[l], 0, None, lam_init, norm_attn[l], w_in[l], lam_q1[l], lam_k1[l],
                            lam_q2[l], lam_k2[l], subln[l], w_out[l], norm_mlp[l], w_up[l], w_down[l],
                            w_ple_gate[l], w_ple_proj[l], ple_norm[l])
        past = (
            cache_k[l, page_table, :, :A_KV_HEADS].reshape(dec_b, past_len, A_KV_HEADS, HEAD_DIM),
            cache_v[l, page_table, :, :A_KV_HEADS].reshape(dec_b, past_len, A_KV_HEADS, HEAD_DIM),
            cache_k[l, page_table, :, A_KV_HEADS:].reshape(dec_b, past_len, B_KV_HEADS, HEAD_DIM),
            cache_v[l, page_table, :, A_KV_HEADS:].reshape(dec_b, past_len, B_KV_HEADS, HEAD_DIM),
        )
        hs, ksm, vsm = _layer(hs, p_sample[l], past_len, past, lam_init, norm_attn[l], w_in[l], lam_q1[l],
                              lam_k1[l], lam_q2[l], lam_k2[l], subln[l], w_out[l], norm_mlp[l], w_up[l],
                              w_down[l], w_ple_gate[l], w_ple_proj[l], ple_norm[l])
        kp_list.append(kp)
        vp_list.append(vp)
        ks_list.append(ksm)
        vs_list.append(vsm)
    y_prompt = _rmsnorm(hp, final_norm)
    y_sample = _rmsnorm(hs, final_norm)
    return (y_prompt, y_sample, jnp.stack(kp_list), jnp.stack(vp_list), jnp.stack(ks_list), jnp.stack(vs_list))
```

```python
import functools
import math

import jax
import jax.numpy as jnp
from jax import lax
from jax.experimental import pallas as pl
from jax.experimental.pallas import tpu as pltpu

F32 = jnp.float32
BF16 = jnp.bfloat16

D_MODEL = 2048
HEAD_DIM = 128
A_HEADS = 8
A_KV_HEADS = 4
A_HALF = 64
B_HEADS = 8
B_KV_HEADS = 4
Q_A = A_HEADS * HEAD_DIM
KV_A = A_KV_HEADS * HEAD_DIM
Q_B = B_HEADS * HEAD_DIM
KV_B = B_KV_HEADS * HEAD_DIM
IN_WIDTH = Q_A + 2 * KV_A + Q_B + 2 * KV_B
D_FF = 4 * D_MODEL
MOBA_BLOCK = 256
MOBA_TOPK = 3
ROPE_THETA = 10000.0
EPS = 1e-6

QA_BLK = 0
QB_BLK = A_HEADS
KA_BLK = QB_BLK + B_HEADS
KB_BLK = KA_BLK + A_KV_HEADS
VA_BLK = KB_BLK + B_KV_HEADS
VB_BLK = VA_BLK + A_KV_HEADS
K_COL = KA_BLK * HEAD_DIM
V_COL = VA_BLK * HEAD_DIM

LANES = 128
QKV_TN = 512
VMEM_LIMIT = 56 * 1024 * 1024

NEG_INF = float("-inf")


def _params(n_axes):
    return pltpu.CompilerParams(
        dimension_semantics=("arbitrary",) * n_axes, vmem_limit_bytes=VMEM_LIMIT)


def _dot_nt(a, b, precision=None):
    return lax.dot_general(a, b, (((1,), (1,)), ((), ())), precision=precision,
                           preferred_element_type=F32)


def _rms(x, gain):
    return x * lax.rsqrt(jnp.mean(x * x, axis=-1, keepdims=True) + EPS) * gain


def _qkv_kernel(x_ref, g_ref, w_ref, ca_ref, sa_ref, cb_ref, sb_ref, o_ref, xn_ref):
    j = pl.program_id(1)

    @pl.when(j == 0)
    def _():
        xn_ref[...] = _rms(x_ref[...], g_ref[...]).astype(BF16)

    acc = jnp.dot(xn_ref[...], w_ref[...], preferred_element_type=F32)
    n_heads = QKV_TN // HEAD_DIM
    tiles_q = Q_A // QKV_TN
    is_diff = (j < tiles_q) | (j == 2 * tiles_q)
    is_moba = ((j >= tiles_q) & (j < 2 * tiles_q)) | (j == 2 * tiles_q + 1)
    is_v = j >= 2 * tiles_q + 2

    @pl.when(is_diff)
    def _():
        lane = lax.broadcasted_iota(jnp.int32, (acc.shape[0], HEAD_DIM), 1)
        first = (lane % A_HALF) < (A_HALF // 2)
        for h in range(n_heads):
            x = acc[:, h * HEAD_DIM:(h + 1) * HEAD_DIM]
            partner = jnp.where(first, pltpu.roll(x, HEAD_DIM - A_HALF // 2, 1),
                                pltpu.roll(x, A_HALF // 2, 1))
            o_ref[:, h * HEAD_DIM:(h + 1) * HEAD_DIM] = x * ca_ref[...] + partner * sa_ref[...]

    @pl.when(is_moba)
    def _():
        for h in range(n_heads):
            x = acc[:, h * HEAD_DIM:(h + 1) * HEAD_DIM]
            partner = pltpu.roll(x, HEAD_DIM // 2, 1)
            o_ref[:, h * HEAD_DIM:(h + 1) * HEAD_DIM] = x * cb_ref[...] + partner * sb_ref[...]

    @pl.when(is_v)
    def _():
        o_ref[...] = acc


def _qkv_proj(x, gain, w, tables, tm, tbl_blocks):
    n = x.shape[0]
    tbl_spec = pl.BlockSpec((tm, HEAD_DIM), lambda i, j: (i % tbl_blocks, 0))
    return pl.pallas_call(
        _qkv_kernel,
        grid=(n // tm, IN_WIDTH // QKV_TN),
        in_specs=[
            pl.BlockSpec((tm, D_MODEL), lambda i, j: (i, 0)),
            pl.BlockSpec((1, D_MODEL), lambda i, j: (0, 0)),
            pl.BlockSpec((D_MODEL, QKV_TN), lambda i, j: (0, j)),
            tbl_spec, tbl_spec, tbl_spec, tbl_spec,
        ],
        out_specs=pl.BlockSpec((tm, QKV_TN), lambda i, j: (i, j)),
        out_shape=jax.ShapeDtypeStruct((n, IN_WIDTH), F32),
        scratch_shapes=[pltpu.VMEM((tm, D_MODEL), BF16)],
        compiler_params=_params(2),
        name="qkv_proj",
    )(x, gain, w, *tables)


def _lambda(lq1_ref, lk1_ref, lq2_ref, lk2_ref, lam_init):
    a = jnp.exp(jnp.sum(lq1_ref[...] * lk1_ref[...], axis=-1, keepdims=True))
    b = jnp.exp(jnp.sum(lq2_ref[...] * lk2_ref[...], axis=-1, keepdims=True))
    return a - b + lam_init


def _split_maps(q):
    lane = lax.broadcasted_iota(jnp.int32, q.shape, 1)
    return jnp.where(lane < A_HALF, q, 0.0), jnp.where(lane >= A_HALF, q, 0.0)


def _diff_prompt_kernel(lq1_ref, lk1_ref, lq2_ref, lk2_ref, sub_ref, q_ref, k_ref, v_ref,
                        o_ref, kb_ref, vb_ref, *, lam_init, tq):
    qi = pl.program_id(2)

    @pl.when(qi == 0)
    def _():
        kb_ref[...] = k_ref[0].astype(BF16)
        vb_ref[...] = v_ref[0].astype(BF16)

    scale = A_HALF ** -0.5
    q1, q2 = _split_maps(q_ref[0])
    qq = jnp.concatenate([q1, q2], axis=0).astype(BF16)

    def update(carry, s, vj):
        m, l, acc = carry
        m_new = jnp.maximum(m, jnp.max(s, axis=-1, keepdims=True))
        alpha = jnp.exp(m - m_new)
        p = jnp.exp(s - m_new)
        l = alpha * l + jnp.sum(p, axis=-1, keepdims=True)
        acc = alpha * acc + jnp.dot(p.astype(BF16), vj, preferred_element_type=F32)
        return m_new, l, acc

    def body(j, carry):
        start = pl.multiple_of(j * tq, tq)
        s = _dot_nt(qq, kb_ref[pl.ds(start, tq), :]) * scale
        return update(carry, s, vb_ref[pl.ds(start, tq), :])

    init = (jnp.full((2 * tq, 1), NEG_INF, F32), jnp.zeros((2 * tq, 1), F32),
            jnp.zeros((2 * tq, HEAD_DIM), F32))
    carry = lax.fori_loop(0, qi, body, init)

    start = pl.multiple_of(qi * tq, tq)
    s = _dot_nt(qq, kb_ref[pl.ds(start, tq), :]) * scale
    row = lax.broadcasted_iota(jnp.int32, (2 * tq, tq), 0) % tq
    col = lax.broadcasted_iota(jnp.int32, (2 * tq, tq), 1)
    s = jnp.where(col <= row, s, NEG_INF)
    _, l, acc = update(carry, s, vb_ref[pl.ds(start, tq), :])

    lam = _lambda(lq1_ref, lk1_ref, lq2_ref, lk2_ref, lam_init)
    o = acc / l
    o = o[:tq] - lam * o[tq:]
    o_ref[0] = (_rms(o, sub_ref[...]) * (1.0 - lam_init)).astype(BF16)


def _diff_prompt(qkv, lams, subln, lam_init, tq):
    b, t, _ = qkv.shape
    vec = pl.BlockSpec((1, A_HALF), lambda bi, h, qi: (0, 0))
    return pl.pallas_call(
        functools.partial(_diff_prompt_kernel, lam_init=lam_init, tq=tq),
        grid=(b, A_HEADS, t // tq),
        in_specs=[
            vec, vec, vec, vec,
            pl.BlockSpec((1, HEAD_DIM), lambda bi, h, qi: (0, 0)),
            pl.BlockSpec((1, tq, HEAD_DIM), lambda bi, h, qi: (bi, qi, QA_BLK + h)),
            pl.BlockSpec((1, t, HEAD_DIM), lambda bi, h, qi: (bi, 0, KA_BLK + h // 2)),
            pl.BlockSpec((1, t, HEAD_DIM), lambda bi, h, qi: (bi, 0, VA_BLK + h // 2)),
        ],
        out_specs=pl.BlockSpec((1, tq, HEAD_DIM), lambda bi, h, qi: (bi, qi, h)),
        out_shape=jax.ShapeDtypeStruct((b, t, Q_A), BF16),
        scratch_shapes=[pltpu.VMEM((t, HEAD_DIM), BF16), pltpu.VMEM((t, HEAD_DIM), BF16)],
        compiler_params=_params(3),
        name="diff_prompt",
    )(*lams, subln, qkv, qkv, qkv)


def _select_blocks(gate, n_cand):
    lane = lax.broadcasted_iota(jnp.int32, gate.shape, 1)
    cand = lane < n_cand
    g = jnp.where(cand, gate, NEG_INF)
    rank = jnp.zeros(gate.shape, jnp.int32)
    for r in range(1, LANES // 16):
        rank += (pltpu.roll(g, r, 1) >= g).astype(jnp.int32)
        rank += (pltpu.roll(g, LANES - r, 1) > g).astype(jnp.int32)
    return cand & (rank < MOBA_TOPK)


def _moba_prompt_kernel(q_ref, k_ref, v_ref, o_ref, kb_ref, vb_ref, km_ref, m_ref, l_ref,
                        acc_ref, *, n_blk):
    qi = pl.program_id(2)
    tq = MOBA_BLOCK

    @pl.when(qi == 0)
    def _():
        kb_ref[...] = k_ref[0].astype(BF16)
        vb_ref[...] = v_ref[0].astype(BF16)
        km_ref[...] = jnp.zeros(km_ref.shape, F32)
        for n in range(n_blk):
            km_ref[n:n + 1, :] = jnp.mean(k_ref[0, n * tq:(n + 1) * tq, :], axis=0,
                                          keepdims=True)

    scale = HEAD_DIM ** -0.5
    q = q_ref[0]
    qb = q.astype(BF16)
    gate = _dot_nt(q, km_ref[...], precision=lax.Precision.HIGHEST)
    sel = _select_blocks(gate, qi)

    start = pl.multiple_of(qi * tq, tq)
    s = _dot_nt(qb, kb_ref[pl.ds(start, tq), :]) * scale
    row = lax.broadcasted_iota(jnp.int32, (tq, tq), 0)
    col = lax.broadcasted_iota(jnp.int32, (tq, tq), 1)
    s = jnp.where(col <= row, s, NEG_INF)
    m = jnp.max(s, axis=-1, keepdims=True)
    p = jnp.exp(s - m)
    m_ref[...] = m
    l_ref[...] = jnp.sum(p, axis=-1, keepdims=True)
    acc_ref[...] = jnp.dot(p.astype(BF16), vb_ref[pl.ds(start, tq), :],
                           preferred_element_type=F32)

    for j in range(n_blk - 1):
        @pl.when(j < qi)
        def _(j=j):
            s = _dot_nt(qb, kb_ref[j * tq:(j + 1) * tq, :]) * scale
            s = jnp.where(sel[:, j:j + 1], s, NEG_INF)
            m_old = m_ref[...]
            m_new = jnp.maximum(m_old, jnp.max(s, axis=-1, keepdims=True))
            alpha = jnp.exp(m_old - m_new)
            p = jnp.exp(s - m_new)
            m_ref[...] = m_new
            l_ref[...] = alpha * l_ref[...] + jnp.sum(p, axis=-1, keepdims=True)
            acc_ref[...] = alpha * acc_ref[...] + jnp.dot(
                p.astype(BF16), vb_ref[j * tq:(j + 1) * tq, :], preferred_element_type=F32)

    o_ref[0] = (acc_ref[...] / l_ref[...]).astype(BF16)


def _moba_prompt(qkv):
    b, t, _ = qkv.shape
    tq = MOBA_BLOCK
    n_blk = t // tq
    return pl.pallas_call(
        functools.partial(_moba_prompt_kernel, n_blk=n_blk),
        grid=(b, B_HEADS, n_blk),
        in_specs=[
            pl.BlockSpec((1, tq, HEAD_DIM), lambda bi, h, qi: (bi, qi, QB_BLK + h)),
            pl.BlockSpec((1, t, HEAD_DIM), lambda bi, h, qi: (bi, 0, KB_BLK + h // 2)),
            pl.BlockSpec((1, t, HEAD_DIM), lambda bi, h, qi: (bi, 0, VB_BLK + h // 2)),
        ],
        out_specs=pl.BlockSpec((1, tq, HEAD_DIM), lambda bi, h, qi: (bi, qi, h)),
        out_shape=jax.ShapeDtypeStruct((b, t, Q_B), BF16),
        scratch_shapes=[
            pltpu.VMEM((t, HEAD_DIM), BF16), pltpu.VMEM((t, HEAD_DIM), BF16),
            pltpu.VMEM((LANES, HEAD_DIM), F32),
            pltpu.VMEM((tq, 1), F32), pltpu.VMEM((tq, 1), F32), pltpu.VMEM((tq, HEAD_DIM), F32),
        ],
        compiler_params=_params(3),
        name="moba_prompt",
    )(qkv, qkv, qkv)


SAMPLE_ROWS = 32
DIFF_ROWS = 4 * A_KV_HEADS
MOBA_ROW0 = DIFF_ROWS
ROW_KV_HEAD = ([r // 4 for r in range(DIFF_ROWS)]
               + [A_KV_HEADS + h // 2 for h in range(B_HEADS)]
               + [0] * (SAMPLE_ROWS - DIFF_ROWS - B_HEADS))


def _sample_attn_kernel(tbl_ref, lq1_ref, lk1_ref, lq2_ref, lk2_ref, sub_ref, qkv_ref, *rest,
                        lam_init, n_pg, n_pages, page):
    del tbl_ref
    k_refs = rest[:n_pg]
    v_refs = rest[n_pg:2 * n_pg]
    o_ref, qm_ref, m_ref, l_ref, acc_ref, pm_ref, pl_ref, pacc_ref, ksum_ref = rest[2 * n_pg:]
    step = pl.program_id(1)
    n_heads = A_KV_HEADS + B_KV_HEADS
    pages_per_blk = MOBA_BLOCK // page
    n_blk = n_pages // pages_per_blk
    rows_keys = page * n_heads

    def head_slice(blk):
        return qkv_ref[0, :, blk * HEAD_DIM:(blk + 1) * HEAD_DIM]

    row = lax.broadcasted_iota(jnp.int32, (SAMPLE_ROWS, rows_keys), 0)
    col = lax.broadcasted_iota(jnp.int32, (SAMPLE_ROWS, rows_keys), 1)
    row_kv = jnp.where(row < DIFF_ROWS, row // 4,
                       jnp.where(row < DIFF_ROWS + B_HEADS,
                                 A_KV_HEADS + (row - DIFF_ROWS) // 2, 0))
    valid = (col % n_heads) == row_kv
    scale = jnp.where(row < DIFF_ROWS, A_HALF ** -0.5, HEAD_DIM ** -0.5).astype(F32)
    running = row[:, :1] < DIFF_ROWS

    @pl.when(step == 0)
    def _():
        parts = []
        for h in range(A_HEADS):
            parts += list(_split_maps(head_slice(QA_BLK + h)))
        parts += [head_slice(QB_BLK + h) for h in range(B_HEADS)]
        parts.append(jnp.zeros((SAMPLE_ROWS - len(parts), HEAD_DIM), F32))
        qm_ref[...] = jnp.concatenate(parts, axis=0).astype(BF16).astype(F32)
        m_ref[...] = jnp.full(m_ref.shape, NEG_INF, F32)
        l_ref[...] = jnp.zeros(l_ref.shape, F32)
        acc_ref[...] = jnp.zeros(acc_ref.shape, F32)
        ksum_ref[...] = jnp.zeros(ksum_ref.shape, F32)

    qm = qm_ref[...].astype(BF16)
    for i in range(n_pg):
        k32 = k_refs[i][...]
        s = _dot_nt(qm, k32.astype(BF16))
        s = jnp.where(valid, s * scale, NEG_INF)
        m_old = m_ref[...]
        p_max = jnp.max(s, axis=-1, keepdims=True)
        m_new = jnp.where(running, jnp.maximum(m_old, p_max), p_max)
        alpha = jnp.where(running, jnp.exp(m_old - m_new), 0.0)
        p = jnp.exp(s - m_new)
        l_new = alpha * l_ref[...] + jnp.sum(p, axis=-1, keepdims=True)
        acc_new = alpha * acc_ref[...] + jnp.dot(
            p.astype(BF16), v_refs[i][...].astype(BF16), preferred_element_type=F32)
        m_ref[...] = m_new
        l_ref[...] = l_new
        acc_ref[...] = acc_new
        off = pl.multiple_of((step * n_pg + i) * B_HEADS, B_HEADS)
        lo, hi = MOBA_ROW0, MOBA_ROW0 + B_HEADS
        pm_ref[pl.ds(off, B_HEADS), :] = jnp.broadcast_to(m_new[lo:hi], (B_HEADS, HEAD_DIM))
        pl_ref[pl.ds(off, B_HEADS), :] = jnp.broadcast_to(l_new[lo:hi], (B_HEADS, HEAD_DIM))
        pacc_ref[pl.ds(off, B_HEADS), :] = acc_new[lo:hi]
        blk = step * (n_pg // pages_per_blk) + i // pages_per_blk
        boff = pl.multiple_of(blk * n_heads, n_heads)
        ksum_ref[pl.ds(boff, n_heads), :] += jnp.sum(
            k32.reshape(page, n_heads, HEAD_DIM), axis=0)

    @pl.when(step == pl.num_programs(1) - 1)
    def _():
        lam = _lambda(lq1_ref, lk1_ref, lq2_ref, lk2_ref, lam_init)
        k_new = [head_slice(KA_BLK + h) for h in range(n_heads)]
        v_new = [head_slice(VA_BLK + h) for h in range(n_heads)]
        k_rows = jnp.concatenate([k_new[h] for h in ROW_KV_HEAD], axis=0)
        v_rows = jnp.concatenate([v_new[h] for h in ROW_KV_HEAD], axis=0)
        s_new = jnp.sum(qm_ref[...] * k_rows.astype(BF16).astype(F32), axis=-1,
                        keepdims=True) * scale[:, :1]

        m_old = m_ref[...]
        m_new = jnp.maximum(m_old, s_new)
        alpha = jnp.exp(m_old - m_new)
        p = jnp.exp(s_new - m_new)
        o = (alpha * acc_ref[...] + p * v_rows) / (alpha * l_ref[...] + p)
        for h in range(A_HEADS):
            og = o[2 * h:2 * h + 1] - lam * o[2 * h + 1:2 * h + 2]
            o_ref[0, :, h * HEAD_DIM:(h + 1) * HEAD_DIM] = (
                _rms(og, sub_ref[...]) * (1.0 - lam_init)).astype(BF16)

        blk_of_pg = lax.broadcasted_iota(jnp.int32, (n_pages, HEAD_DIM), 0) // pages_per_blk
        cand = lax.broadcasted_iota(jnp.int32, (8, n_blk), 1)
        for kh in range(B_KV_HEADS):
            km = ksum_ref[pl.ds(A_KV_HEADS + kh, n_blk, stride=n_heads), :] * (1.0 / MOBA_BLOCK)
            qb = jnp.concatenate([head_slice(QB_BLK + 2 * kh), head_slice(QB_BLK + 2 * kh + 1),
                                  jnp.zeros((6, HEAD_DIM), F32)], axis=0)
            gate = _dot_nt(qb, km, precision=lax.Precision.HIGHEST)
            picks = []
            for _ in range(MOBA_TOPK):
                mx = jnp.max(gate, axis=-1, keepdims=True)
                idx = jnp.min(jnp.where(gate == mx, cand, n_blk - 1), axis=-1, keepdims=True)
                gate = jnp.where(cand == idx, NEG_INF, gate)
                picks.append(idx)
            for g in range(2):
                h = 2 * kh + g
                sel = blk_of_pg == picks[0][g:g + 1]
                for idx in picks[1:]:
                    sel = sel | (blk_of_pg == idx[g:g + 1])
                m_pg = pm_ref[pl.ds(h, n_pages, stride=B_HEADS), :]
                l_pg = pl_ref[pl.ds(h, n_pages, stride=B_HEADS), :]
                a_pg = pacc_ref[pl.ds(h, n_pages, stride=B_HEADS), :]
                s_own = s_new[MOBA_ROW0 + h:MOBA_ROW0 + h + 1]
                top = jnp.maximum(
                    jnp.max(jnp.where(sel, m_pg, NEG_INF), axis=0, keepdims=True), s_own)
                w = jnp.where(sel, jnp.exp(m_pg - top), 0.0)
                p_own = jnp.exp(s_own - top)
                num = jnp.sum(w * a_pg, axis=0, keepdims=True) + p_own * v_new[A_KV_HEADS + kh]
                den = jnp.sum(w * l_pg, axis=0, keepdims=True) + p_own
                o_ref[0, :, Q_A + h * HEAD_DIM:Q_A + (h + 1) * HEAD_DIM] = (num / den).astype(BF16)


def _sample_attn(page_table, lams, subln, qkv_s, cache_k, cache_v, layer, lam_init, n_pg):
    dec_b, n_pages = page_table.shape
    rows_keys = cache_k.shape[2]
    n_heads = A_KV_HEADS + B_KV_HEADS
    page = rows_keys // n_heads
    n_blk = n_pages * page // MOBA_BLOCK
    vec = pl.BlockSpec((1, A_HALF), lambda b, s, tbl: (0, 0))

    def page_spec(i):
        return pl.BlockSpec((None, None, rows_keys, HEAD_DIM),
                            lambda b, s, tbl: (layer, tbl[b, s * n_pg + i], 0, 0))

    grid_spec = pltpu.PrefetchScalarGridSpec(
        num_scalar_prefetch=1,
        grid=(dec_b, n_pages // n_pg),
        in_specs=[vec, vec, vec, vec,
                  pl.BlockSpec((1, HEAD_DIM), lambda b, s, tbl: (0, 0)),
                  pl.BlockSpec((1, 1, IN_WIDTH), lambda b, s, tbl: (b, 0, 0))]
                 + [page_spec(i % n_pg) for i in range(2 * n_pg)],
        out_specs=pl.BlockSpec((1, 1, Q_A + Q_B), lambda b, s, tbl: (b, 0, 0)),
        scratch_shapes=[
            pltpu.VMEM((SAMPLE_ROWS, HEAD_DIM), F32),
            pltpu.VMEM((SAMPLE_ROWS, 1), F32), pltpu.VMEM((SAMPLE_ROWS, 1), F32),
            pltpu.VMEM((SAMPLE_ROWS, HEAD_DIM), F32),
            pltpu.VMEM((n_pages * B_HEADS, HEAD_DIM), F32),
            pltpu.VMEM((n_pages * B_HEADS, HEAD_DIM), F32),
            pltpu.VMEM((n_pages * B_HEADS, HEAD_DIM), F32),
            pltpu.VMEM((n_blk * n_heads, HEAD_DIM), F32),
        ],
    )
    return pl.pallas_call(
        functools.partial(_sample_attn_kernel, lam_init=lam_init, n_pg=n_pg, n_pages=n_pages,
                          page=page),
        grid_spec=grid_spec,
        out_shape=jax.ShapeDtypeStruct((dec_b, 1, Q_A + Q_B), BF16),
        compiler_params=_params(2),
        name="sample_attn",
    )(page_table, *lams, subln, qkv_s, *([cache_k] * n_pg), *([cache_v] * n_pg))


def _out_proj_kernel(x_ref, a_ref, b_ref, w_ref, o_ref):
    o_ref[...] = (x_ref[...]
                  + jnp.dot(a_ref[...], w_ref[:Q_A, :], preferred_element_type=F32)
                  + jnp.dot(b_ref[...], w_ref[Q_A:, :], preferred_element_type=F32))


def _out_proj(x, oa, ob, w, tm, tn):
    n = x.shape[0]
    return pl.pallas_call(
        _out_proj_kernel,
        grid=(n // tm, D_MODEL // tn),
        in_specs=[
            pl.BlockSpec((tm, tn), lambda i, j: (i, j)),
            pl.BlockSpec((tm, Q_A), lambda i, j: (i, 0)),
            pl.BlockSpec((tm, Q_B), lambda i, j: (i, 0)),
            pl.BlockSpec((Q_A + Q_B, tn), lambda i, j: (0, j)),
        ],
        out_specs=pl.BlockSpec((tm, tn), lambda i, j: (i, j)),
        out_shape=jax.ShapeDtypeStruct((n, D_MODEL), F32),
        compiler_params=_params(2),
        name="out_proj",
    )(x, oa, ob, w)


def _mlp_kernel(x_ref, g_ref, wu_ref, wd_ref, o_ref, xn_ref, acc_ref):
    f = pl.program_id(1)

    @pl.when(f == 0)
    def _():
        xn_ref[...] = _rms(x_ref[...], g_ref[...]).astype(BF16)
        acc_ref[...] = jnp.zeros(acc_ref.shape, F32)

    h = jnp.maximum(jnp.dot(xn_ref[...], wu_ref[...], preferred_element_type=F32), 0.0)
    acc_ref[...] += jnp.dot((h * h).astype(BF16), wd_ref[...], preferred_element_type=F32)

    @pl.when(f == pl.num_programs(1) - 1)
    def _():
        o_ref[...] = x_ref[...] + acc_ref[...]


def _mlp(x, gain, w_up, w_down, tm, tf):
    n = x.shape[0]
    return pl.pallas_call(
        _mlp_kernel,
        grid=(n // tm, D_FF // tf),
        in_specs=[
            pl.BlockSpec((tm, D_MODEL), lambda i, f: (i, 0)),
            pl.BlockSpec((1, D_MODEL), lambda i, f: (0, 0)),
            pl.BlockSpec((D_MODEL, tf), lambda i, f: (0, f)),
            pl.BlockSpec((tf, D_MODEL), lambda i, f: (f, 0)),
        ],
        out_specs=pl.BlockSpec((tm, D_MODEL), lambda i, f: (i, 0)),
        out_shape=jax.ShapeDtypeStruct((n, D_MODEL), F32),
        scratch_shapes=[pltpu.VMEM((tm, D_MODEL), BF16), pltpu.VMEM((tm, D_MODEL), F32)],
        compiler_params=_params(2),
        name="mlp",
    )(x, gain, w_up, w_down)


def _ple_kernel(x_ref, p_ref, wg_ref, wp_ref, gn_ref, fn_ref, o_ref, *, final):
    x = x_ref[...]
    proj = jnp.dot(p_ref[...].astype(BF16), wp_ref[...], preferred_element_type=F32)
    z = jnp.dot(x.astype(BF16), wg_ref[...], preferred_element_type=F32)
    y = x + (1.0 / (1.0 + jnp.exp(-z))) * _rms(proj, gn_ref[...])
    o_ref[...] = _rms(y, fn_ref[...]) if final else y


def _ple(x, p, w_gate, w_proj, gain, final_gain, final, tm):
    n = x.shape[0]
    ple_dim = p.shape[1]
    return pl.pallas_call(
        functools.partial(_ple_kernel, final=final),
        grid=(n // tm,),
        in_specs=[
            pl.BlockSpec((tm, D_MODEL), lambda i: (i, 0)),
            pl.BlockSpec((tm, ple_dim), lambda i: (i, 0)),
            pl.BlockSpec((D_MODEL, D_MODEL), lambda i: (0, 0)),
            pl.BlockSpec((ple_dim, D_MODEL), lambda i: (0, 0)),
            pl.BlockSpec((1, D_MODEL), lambda i: (0, 0)),
            pl.BlockSpec((1, D_MODEL), lambda i: (0, 0)),
        ],
        out_specs=pl.BlockSpec((tm, D_MODEL), lambda i: (i, 0)),
        out_shape=jax.ShapeDtypeStruct((n, D_MODEL), F32),
        compiler_params=_params(1),
        name="ple",
    )(x, p, w_gate, w_proj, gain, final_gain)


def _rope_tables(pos):
    lane = jnp.arange(HEAD_DIM)
    out = []
    for dim in (A_HALF, HEAD_DIM):
        half = dim // 2
        inv = jnp.exp(jnp.arange(half, dtype=F32) * (-2.0 * math.log(ROPE_THETA) / dim))
        ang = pos.astype(F32)[:, None] * inv[None, :]
        idx = lane % half
        sign = jnp.where((lane % dim) < half, -1.0, 1.0).astype(F32)
        out += [jnp.cos(ang)[:, idx], jnp.sin(ang)[:, idx] * sign[None, :]]
    return out


def _permute_w_in(w):
    qa, ka, va, qb, kb, vb = jnp.split(
        w, [Q_A, Q_A + KV_A, Q_A + 2 * KV_A, Q_A + 2 * KV_A + Q_B,
            Q_A + 2 * KV_A + Q_B + KV_B], axis=-1)
    return jnp.concatenate([qa, qb, ka, kb, va, vb], axis=-1)


def kernel(x_prompt, x_sample, p_prompt, p_sample, cache_k, cache_v, page_table, norm_attn, w_in,
           lam_q1, lam_k1, lam_q2, lam_k2, subln, w_out, norm_mlp, w_up, w_down, w_ple_gate,
           w_ple_proj, ple_norm, final_norm):
    depth = w_in.shape[0]
    batch, seq, _ = x_prompt.shape
    dec_b, dec_t, _ = x_sample.shape
    assert dec_t == 1 and seq % MOBA_BLOCK == 0
    n_pool, page = cache_k.shape[1], cache_k.shape[2]
    n_pages = page_table.shape[1]
    past_len = n_pages * page
    assert past_len % MOBA_BLOCK == 0 and MOBA_BLOCK % page == 0
    n_p, n_s = batch * seq, dec_b * dec_t

    tm_p = 512
    tables_p = _rope_tables(jnp.arange(seq))
    tables_s = _rope_tables(jnp.full((n_s,), past_len))
    ck = cache_k.reshape(depth, n_pool, page * (A_KV_HEADS + B_KV_HEADS), HEAD_DIM)
    cv = cache_v.reshape(depth, n_pool, page * (A_KV_HEADS + B_KV_HEADS), HEAD_DIM)
    fin = final_norm.reshape(1, D_MODEL)

    hp = x_prompt.reshape(n_p, D_MODEL)
    hs = x_sample.reshape(n_s, D_MODEL)
    kp, vp, ks, vs = [], [], [], []
    for l in range(depth):
        lam_init = 0.8 - 0.6 * math.exp(-0.3 * l)
        last = l == depth - 1
        w_qkv = _permute_w_in(w_in[l]).astype(BF16)
        wo = w_out[l].astype(BF16)
        wu = w_up[l].astype(BF16)
        wd = w_down[l].astype(BF16)
        wg = w_ple_gate[l].astype(BF16)
        wp = w_ple_proj[l].astype(BF16)
        g_attn = norm_attn[l].reshape(1, D_MODEL)
        g_mlp = norm_mlp[l].reshape(1, D_MODEL)
        g_ple = ple_norm[l].reshape(1, D_MODEL)
        sub = subln[l].reshape(1, HEAD_DIM)
        lams = [v[l].reshape(1, A_HALF) for v in (lam_q1, lam_k1, lam_q2, lam_k2)]

        qkv = _qkv_proj(hp, g_attn, w_qkv, tables_p, tm_p, seq // tm_p)
        kp.append(qkv[:, K_COL:V_COL].reshape(batch, seq, -1, HEAD_DIM))
        vp.append(qkv[:, V_COL:].reshape(batch, seq, -1, HEAD_DIM))
        qkv3 = qkv.reshape(batch, seq, IN_WIDTH)
        oa = _diff_prompt(qkv3, lams, sub, lam_init, 256).reshape(n_p, Q_A)
        ob = _moba_prompt(qkv3).reshape(n_p, Q_B)
        hp = _out_proj(hp, oa, ob, wo, tm_p, 512)
        hp = _mlp(hp, g_mlp, wu, wd, tm_p, 512)
        hp = _ple(hp, p_prompt[l].reshape(n_p, -1), wg, wp, g_ple, fin, last, 256)

        qkv = _qkv_proj(hs, g_attn, w_qkv, tables_s, n_s, 1)
        ks.append(qkv[:, K_COL:V_COL].reshape(dec_b, dec_t, -1, HEAD_DIM))
        vs.append(qkv[:, V_COL:].reshape(dec_b, dec_t, -1, HEAD_DIM))
        qkv3 = qkv.reshape(dec_b, dec_t, IN_WIDTH)
        o = _sample_attn(page_table, lams, sub, qkv3, ck, cv, l, lam_init, 8).reshape(n_s, -1)
        hs = _out_proj(hs, o[:, :Q_A], o[:, Q_A:], wo, n_s, 512)
        hs = _mlp(hs, g_mlp, wu, wd, n_s, 512)
        hs = _ple(hs, p_sample[l].reshape(n_s, -1), wg, wp, g_ple, fin, last, n_s)

    return (hp.reshape(batch, seq, D_MODEL), hs.reshape(dec_b, dec_t, D_MODEL),
            jnp.stack(kp), jnp.stack(vp), jnp.stack(ks), jnp.stack(vs))
```
